```python
import math
import jax
import jax.numpy as jnp
from jax import lax
import numpy as np

D_MODEL = 1024
BATCH = 1
SEQ = 16384
DEPTH = 1

CHUNK = 64
Q_BLOCK = 128
A_HEADS = 8
A_HEAD_DIM = 64
B_HEADS = 8
B_HEAD_DIM = 128
IDX_HEADS = 16
IDX_DIM = 64
TOPK_MAX = 256
N_EXPERTS = 32
TOP_K = 4
D_FF = 1024
SWIGLU_ALPHA = 1.702
SWIGLU_LIMIT = 7.0
MOE_BLOCK = 128
LN_EPS = 1e-5
DEEPNORM_ALPHA = (2.0 * DEPTH) ** 0.25
DEEPNORM_BETA = (8.0 * DEPTH) ** -0.25

A_QK = A_HEADS * 2 * A_HEAD_DIM
A_V = A_HEADS * 2 * A_HEAD_DIM
B_W = B_HEADS * B_HEAD_DIM
IDX_Q = IDX_HEADS * IDX_DIM
SPLIT_SIZES = (A_QK, A_QK, A_V, B_W, B_W, B_W, IDX_Q, IDX_DIM, IDX_HEADS, D_MODEL, D_MODEL)
SPLIT_POINTS = tuple(int(v) for v in np.cumsum(SPLIT_SIZES)[:-1])
PROJ_WIDTH = int(sum(SPLIT_SIZES))

kernel_name = 'hybrid_diffattn_dsa_moe_block'


def alibi_slopes(n):
    return 2.0 ** (-8.0 * jnp.arange(1, n + 1, dtype=jnp.float32) / n)


def chunk_end(pos):
    return (pos // CHUNK + 1) * CHUNK


def layer_norm(x, g=None, b=None):
    xf = x.astype(jnp.float32)
    mu = jnp.mean(xf, axis=-1, keepdims=True)
    var = jnp.mean(jnp.square(xf - mu), axis=-1, keepdims=True)
    y = (xf - mu) * lax.rsqrt(var + LN_EPS)
    if g is not None:
        y = y * g.astype(jnp.float32) + b.astype(jnp.float32)
    return y.astype(x.dtype)


def diff_attention(q, k, v, lam, lam_init, norm_g):
    B, S, H, _, d = q.shape
    nb = S // Q_BLOCK
    slopes = alibi_slopes(H)
    kpos = jnp.arange(S)
    scale = d ** -0.5

    def block(i):
        q_blk = lax.dynamic_slice_in_dim(q, i * Q_BLOCK, Q_BLOCK, axis=1)
        qpos = i * Q_BLOCK + jnp.arange(Q_BLOCK)
        s = jnp.einsum('bqhcd,bkhcd->bhcqk', q_blk, k, preferred_element_type=jnp.float32) * scale
        dist = jnp.abs(qpos[:, None] - kpos[None, :]).astype(jnp.float32)
        allowed = kpos[None, :] < chunk_end(qpos)[:, None]
        s = jnp.where(allowed, s - slopes[:, None, None, None] * dist, -jnp.inf)
        p = jax.nn.softmax(s, axis=-1)
        w = p[:, :, 0] - lam * p[:, :, 1]
        return jnp.einsum('bhqk,bkhe->bqhe', w.astype(v.dtype), v)

    o = lax.map(block, jnp.arange(nb))
    o = jnp.moveaxis(o, 0, 1).reshape(B, S, H, 2 * d)
    of = o.astype(jnp.float32)
    of = of * lax.rsqrt(jnp.mean(jnp.square(of), axis=-1, keepdims=True) + LN_EPS)
    of = of * norm_g.astype(jnp.float32) * (1.0 - lam_init)
    return of.astype(v.dtype).reshape(B, S, H * 2 * d)


def dsa_attention(q, k, v, qi, ki, wi):
    B, S, H, E = q.shape
    nb = S // Q_BLOCK
    topk = min(TOPK_MAX, S // 4)
    slopes = alibi_slopes(H)
    kpos = jnp.arange(S)
    gather = jax.vmap(lambda a, j: a[j])

    def block(i):
        sl = lambda a: lax.dynamic_slice_in_dim(a, i * Q_BLOCK, Q_BLOCK, axis=1)
        q_blk, qi_blk, wi_blk = sl(q), sl(qi), sl(wi)
        qpos = i * Q_BLOCK + jnp.arange(Q_BLOCK)
        end = chunk_end(qpos)
        allowed = kpos[None, :] < end[:, None]
        logits = jnp.einsum('bqgd,bkd->bqgk', qi_blk, ki, preferred_element_type=jnp.float32) * (IDX_DIM ** -0.5)
        w_h = wi_blk.astype(jnp.float32) * (IDX_HEADS ** -0.5)
        score = jnp.einsum('bqg,bqgk->bqk', w_h, jax.nn.relu(logits))
        score = jnp.where(allowed, score, -jnp.inf)
        _, idx = lax.top_k(score, topk)
        valid = idx < end[None, :, None]
        k_sel = gather(k, idx)
        v_sel = gather(v, idx)
        s = jnp.einsum('bqhe,bqkhe->bhqk', q_blk, k_sel, preferred_element_type=jnp.float32) * (E ** -0.5)
        dist = jnp.abs(qpos[None, :, None] - idx).astype(jnp.float32)
        s = s - slopes[None, :, None, None] * dist[:, None]
        s = jnp.where(valid[:, None], s, -jnp.inf)
        p = jax.nn.softmax(s, axis=-1)
        return jnp.einsum('bhqk,bqkhe->bqhe', p.astype(v.dtype), v_sel)

    o = lax.map(block, jnp.arange(nb))
    return jnp.moveaxis(o, 0, 1).reshape(B, S, H * E)


def moe_ffn(h, w_r, b_r, w1, b1, w2, b2):
    B, S, D = h.shape
    t = h.reshape(-1, D)
    logits = jnp.matmul(t, w_r, preferred_element_type=jnp.float32) + b_r.astype(jnp.float32)
    vals, idx = lax.top_k(logits, TOP_K)
    wts = jax.nn.softmax(vals, axis=-1)
    gate = jnp.einsum('nk,nke->ne', wts, jax.nn.one_hot(idx, N_EXPERTS, dtype=jnp.float32)).astype(t.dtype)
    n = t.shape[0]
    nb = n // MOE_BLOCK
    tb = t.reshape(nb, MOE_BLOCK, D)
    gb = gate.reshape(nb, MOE_BLOCK, N_EXPERTS)

    def block(args):
        xb, g = args
        u = jnp.einsum('td,edf->tef', xb, w1) + b1
        x_glu = jnp.minimum(u[..., 0::2], SWIGLU_LIMIT)
        x_lin = jnp.clip(u[..., 1::2], -SWIGLU_LIMIT, SWIGLU_LIMIT)
        a = x_glu * jax.nn.sigmoid(SWIGLU_ALPHA * x_glu) * (x_lin + 1.0)
        a = a * g[:, :, None]
        return jnp.einsum('tef,efd->td', a, w2) + jnp.matmul(g, b2)

    y = lax.map(block, (tb, gb))
    return y.reshape(B, S, D)


def setup_inputs(seed: int = 0) -> dict:
    key = jax.random.key(seed)
    ks = jax.random.split(key, 24)
    D = D_MODEL
    nrm = lambda k, shape, s: jax.random.normal(k, shape, jnp.float32) * s
    return {
        'x': nrm(ks[0], (BATCH, SEQ, D), 1.0),
        'c': nrm(ks[1], (BATCH, D), 1.0),
        'w_ada': nrm(ks[2], (DEPTH, D, 6 * D), 0.5 * D ** -0.5),
        'b_ada': nrm(ks[3], (DEPTH, 6 * D), 0.02),
        'w_in': nrm(ks[4], (DEPTH, D, PROJ_WIDTH), D ** -0.5),
        'lam_q1': nrm(ks[5], (DEPTH, A_HEAD_DIM), 0.1),
        'lam_k1': nrm(ks[6], (DEPTH, A_HEAD_DIM), 0.1),
        'lam_q2': nrm(ks[7], (DEPTH, A_HEAD_DIM), 0.1),
        'lam_k2': nrm(ks[8], (DEPTH, A_HEAD_DIM), 0.1),
        'diff_norm_g': 1.0 + nrm(ks[9], (DEPTH, 2 * A_HEAD_DIM), 0.02),
        'w_branch_a': nrm(ks[10], (DEPTH, A_V, D), A_V ** -0.5),
        'w_branch_b': nrm(ks[11], (DEPTH, B_W, D), B_W ** -0.5),
        'w_out': nrm(ks[12], (DEPTH, D, D), DEEPNORM_BETA * D ** -0.5),
        'ln1_g': 1.0 + nrm(ks[13], (DEPTH, D), 0.02),
        'ln1_b': nrm(ks[14], (DEPTH, D), 0.02),
        'w_router': nrm(ks[15], (DEPTH, D, N_EXPERTS), D ** -0.5),
        'b_router': nrm(ks[16], (DEPTH, N_EXPERTS), 0.01),
        'w_e1': nrm(ks[17], (DEPTH, N_EXPERTS, D, 2 * D_FF), D ** -0.5),
        'b_e1': nrm(ks[18], (DEPTH, N_EXPERTS, 2 * D_FF), 0.01),
        'w_e2': nrm(ks[19], (DEPTH, N_EXPERTS, D_FF, D), DEEPNORM_BETA * D_FF ** -0.5),
        'b_e2': nrm(ks[20], (DEPTH, N_EXPERTS, D), 0.01),
        'ln2_g': 1.0 + nrm(ks[21], (DEPTH, D), 0.02),
        'ln2_b': nrm(ks[22], (DEPTH, D), 0.02),
    }


def reference(x, c, w_ada, b_ada, w_in, lam_q1, lam_k1, lam_q2, lam_k2, diff_norm_g,
              w_branch_a, w_branch_b, w_out, ln1_g, ln1_b, w_router, b_router,
              w_e1, b_e1, w_e2, b_e2, ln2_g, ln2_b):
    B, S, D = x.shape
    c_act = jax.nn.silu(c)
    for l in range(DEPTH):
        mod = jnp.matmul(c_act, w_ada[l]) + b_ada[l]
        sh_a, sc_a, g_a, sh_f, sc_f, g_f = [m[:, None, :] for m in jnp.split(mod, 6, axis=-1)]

        u = layer_norm(x) * (1.0 + sc_a) + sh_a
        proj = jnp.matmul(u, w_in[l])
        (qa, ka, va, qb, kb, vb, qi, ki, wi, ga, gb) = jnp.split(proj, SPLIT_POINTS, axis=-1)

        lam_init = 0.8 - 0.6 * math.exp(-0.3 * l)
        lam = (jnp.exp(jnp.sum(lam_q1[l].astype(jnp.float32) * lam_k1[l].astype(jnp.float32)))
               - jnp.exp(jnp.sum(lam_q2[l].astype(jnp.float32) * lam_k2[l].astype(jnp.float32)))
               + lam_init)
        ya = diff_attention(qa.reshape(B, S, A_HEADS, 2, A_HEAD_DIM),
                            ka.reshape(B, S, A_HEADS, 2, A_HEAD_DIM),
                            va.reshape(B, S, A_HEADS, 2 * A_HEAD_DIM),
                            lam, lam_init, diff_norm_g[l])
        yb = dsa_attention(qb.reshape(B, S, B_HEADS, B_HEAD_DIM),
                           kb.reshape(B, S, B_HEADS, B_HEAD_DIM),
                           vb.reshape(B, S, B_HEADS, B_HEAD_DIM),
                           qi.reshape(B, S, IDX_HEADS, IDX_DIM), ki, wi)
        merged = (jax.nn.sigmoid(ga) * jnp.matmul(ya, w_branch_a[l])
                  + jax.nn.sigmoid(gb) * jnp.matmul(yb, w_branch_b[l]))
        mix_out = jnp.matmul(merged, w_out[l])
        x = layer_norm(DEEPNORM_ALPHA * x + g_a * mix_out, ln1_g[l], ln1_b[l])

        v = layer_norm(x) * (1.0 + sc_f) + sh_f
        ffn_out = moe_ffn(v, w_router[l], b_router[l], w_e1[l], b_e1[l], w_e2[l], b_e2[l])
        x = layer_norm(DEEPNORM_ALPHA * x + g_f * ffn_out.astype(x.dtype), ln2_g[l], ln2_b[l])
    return x
```

```python
import functools
import math

import jax
import jax.numpy as jnp
from jax import lax
from jax.experimental import pallas as pl
from jax.experimental.pallas import tpu as pltpu

F32 = jnp.float32
BF16 = jnp.bfloat16

CHUNK = 64
A_HEADS = 8
A_HEAD_DIM = 64
B_HEADS = 8
B_HEAD_DIM = 128
IDX_HEADS = 16
IDX_DIM = 64
TOPK_MAX = 256
N_EXPERTS = 32
TOP_K = 4
SWIGLU_ALPHA = 1.702
SWIGLU_LIMIT = 7.0
LN_EPS = 1e-5
DEPTH = 1
DEEPNORM_ALPHA = (2.0 * DEPTH) ** 0.25

LANES = 128
NEG_BIG = -1e30
KEY_SIGN = -2147483648
KEY_MANT = 2147483647
KEY_MIN_FINITE = -2139095040
VMEM_LIMIT = 56 * 1024 * 1024


def _cparams(sem):
    return pltpu.CompilerParams(dimension_semantics=sem, vmem_limit_bytes=VMEM_LIMIT)


def _dot_nt(a, b):
    return lax.dot_general(a, b, (((1,), (1,)), ((), ())), preferred_element_type=F32)


def _layer_norm_rows(x):
    mu = jnp.mean(x, axis=-1, keepdims=True)
    xc = x - mu
    var = jnp.mean(xc * xc, axis=-1, keepdims=True)
    return xc * lax.rsqrt(var + LN_EPS)


def _ln_mod_matmul_kernel(x_ref, sc_ref, sh_ref, w_ref, o_ref, u_ref):
    @pl.when(pl.program_id(1) == 0)
    def _():
        y = _layer_norm_rows(x_ref[...])
        u_ref[...] = (y * (1.0 + sc_ref[...]) + sh_ref[...]).astype(BF16)

    o_ref[...] = jnp.dot(u_ref[...], w_ref[...], preferred_element_type=F32).astype(o_ref.dtype)


def _ln_mod_matmul(x, sc, sh, w, out_dtype, tm, tn):
    s, d = x.shape
    n = w.shape[1]
    return pl.pallas_call(
        _ln_mod_matmul_kernel,
        grid=(s // tm, n // tn),
        in_specs=[
            pl.BlockSpec((tm, d), lambda i, j: (i, 0)),
            pl.BlockSpec((1, d), lambda i, j: (0, 0)),
            pl.BlockSpec((1, d), lambda i, j: (0, 0)),
            pl.BlockSpec((d, tn), lambda i, j: (0, j)),
        ],
        out_specs=pl.BlockSpec((tm, tn), lambda i, j: (i, j)),
        out_shape=jax.ShapeDtypeStruct((s, n), out_dtype),
        scratch_shapes=[pltpu.VMEM((tm, d), BF16)],
        compiler_params=_cparams(("parallel", "arbitrary")),
        name="ln_mod_proj",
    )(x, sc, sh, w)


def _diff_attn_kernel(slope_ref, lam_ref, q_ref, k_ref, v_ref, g_ref, o_ref,
                      m_ref, l_ref, acc_ref, *, tq, lam_init):
    h = pl.program_id(0)
    i = pl.program_id(1)
    slope = slope_ref[h]
    lam = lam_ref[0]

    q = q_ref[...]
    lane = lax.broadcasted_iota(jnp.int32, q.shape, 1)
    qs = q * jnp.asarray(A_HEAD_DIM ** -0.5, BF16)
    zero = jnp.zeros_like(qs)
    q_maps = (jnp.where(lane < A_HEAD_DIM, qs, zero), jnp.where(lane >= A_HEAD_DIM, qs, zero))

    row = lax.broadcasted_iota(jnp.int32, (tq, tq), 0)
    col = lax.broadcasted_iota(jnp.int32, (tq, tq), 1)
    rel = (row - col).astype(F32)
    bias_off = (-slope) * rel
    bias_diag = (-slope) * jnp.abs(rel)
    allowed = col < (row // CHUNK + 1) * CHUNK

    m_ref[...] = jnp.full(m_ref.shape, NEG_BIG, F32)
    l_ref[...] = jnp.zeros(l_ref.shape, F32)
    acc_ref[...] = jnp.zeros(acc_ref.shape, F32)

    def step(j, diag):
        start = pl.multiple_of(j * tq, tq)
        k = k_ref[pl.ds(start, tq), :]
        v = v_ref[pl.ds(start, tq), :]
        shift = (-slope) * ((i - j) * tq).astype(F32)
        for c in range(2):
            s = _dot_nt(q_maps[c], k)
            if diag:
                s = jnp.where(allowed, s + bias_diag, NEG_BIG)
            else:
                s = s + bias_off
            m_old = m_ref[c]
            m_new = jnp.maximum(m_old, jnp.max(s, axis=-1, keepdims=True) + shift)
            alpha = jnp.exp(m_old - m_new)
            p = jnp.exp(s - (m_new - shift))
            l_ref[c] = alpha * l_ref[c] + jnp.sum(p, axis=-1, keepdims=True)
            acc_ref[c] = alpha * acc_ref[c] + jnp.dot(p.astype(BF16), v, preferred_element_type=F32)
            m_ref[c] = m_new

    def body(j, carry):
        step(j, False)
        return carry

    lax.fori_loop(0, i, body, 0)
    step(i, True)

    o = acc_ref[0] / l_ref[0] - lam * (acc_ref[1] / l_ref[1])
    o = o * lax.rsqrt(jnp.mean(o * o, axis=-1, keepdims=True) + LN_EPS)
    o = o * g_ref[...] * (1.0 - lam_init)
    o_ref[...] = o.astype(o_ref.dtype)


def _diff_attention(proj, slopes, lam, norm_g, lam_init, tq):
    s = proj.shape[0]
    hd = 2 * A_HEAD_DIM
    kernel = functools.partial(_diff_attn_kernel, tq=tq, lam_init=lam_init)
    return pl.pallas_call(
        kernel,
        grid=(A_HEADS, s // tq),
        in_specs=[
            pl.BlockSpec(memory_space=pltpu.SMEM),
            pl.BlockSpec(memory_space=pltpu.SMEM),
            pl.BlockSpec((tq, hd), lambda h, i: (i, h)),
            pl.BlockSpec((s, hd), lambda h, i: (0, A_HEADS + h)),
            pl.BlockSpec((s, hd), lambda h, i: (0, 2 * A_HEADS + h)),
            pl.BlockSpec((1, hd), lambda h, i: (0, 0)),
        ],
        out_specs=pl.BlockSpec((tq, hd), lambda h, i: (i, h)),
        out_shape=jax.ShapeDtypeStruct((s, A_HEADS * hd), BF16),
        scratch_shapes=[
            pltpu.VMEM((2, tq, 1), F32),
            pltpu.VMEM((2, tq, 1), F32),
            pltpu.VMEM((2, tq, hd), F32),
        ],
        compiler_params=_cparams(("parallel", "arbitrary")),
        name="diff_attn",
    )(slopes, lam, proj, proj, proj, norm_g)


def _key_to_float(key):
    bits = jnp.where(key >= 0, key, key ^ KEY_MANT)
    return lax.bitcast_convert_type(bits, F32)


def _indexer_kernel(qi_ref, kw_ref, ki_ref, s_ref, thr_ref, *, tq, tk, topk):
    i = pl.program_id(0)
    n_all = s_ref.shape[1]
    n_tiles = ((i + 1) * tq + tk - 1) // tk

    w = kw_ref[:, IDX_DIM:IDX_DIM + IDX_HEADS]
    q_heads = [qi_ref[:, h * IDX_DIM:(h + 1) * IDX_DIM] for h in range(IDX_HEADS)]
    w_heads = [w[:, h:h + 1] for h in range(IDX_HEADS)]

    qpos = i * tq + lax.broadcasted_iota(jnp.int32, (tq, tk), 0)
    qend = (qpos // CHUNK + 1) * CHUNK
    kcol = lax.broadcasted_iota(jnp.int32, (tq, tk), 1)

    def score_tile(jb, carry):
        start = pl.multiple_of(jb * tk, tk)
        kt = ki_ref[pl.ds(start, tk), :]
        acc = jnp.zeros((tq, tk), F32)
        for h in range(IDX_HEADS):
            acc = acc + w_heads[h] * jnp.maximum(_dot_nt(q_heads[h], kt), 0.0)
        s_ref[0, jb] = acc
        return carry

    lax.fori_loop(0, n_tiles - 1, score_tile, 0)
    last = n_tiles - 1
    score_tile(last, 0)
    s_ref[0, last] = jnp.where(kcol + last * tk < qend, s_ref[0, last], -jnp.inf)

    def fill_tile(jb, carry):
        s_ref[0, jb] = jnp.full((tq, tk), -jnp.inf, F32)
        return carry

    lax.fori_loop(n_tiles, n_all, fill_tile, 0)

    def count_ge(cand):
        cb = jnp.broadcast_to(cand, (tq, LANES))

        def body(jb, acc):
            t = s_ref[0, jb]
            for c in range(tk // LANES):
                acc = acc + jnp.where(t[:, c * LANES:(c + 1) * LANES] >= cb, 1.0, 0.0)
            return acc

        acc = lax.fori_loop(0, n_tiles, body, jnp.zeros((tq, LANES), F32))
        return jnp.sum(acc, axis=-1, keepdims=True)

    def bit_step(b, u):
        trial = u | jnp.left_shift(jnp.int32(1), 31 - b)
        cnt = count_ge(_key_to_float(trial ^ KEY_SIGN))
        return jnp.where(cnt >= float(topk), trial, u)

    u = lax.fori_loop(0, 32, bit_step, jnp.zeros((tq, 1), jnp.int32))
    key = jnp.maximum(u ^ KEY_SIGN, KEY_MIN_FINITE)
    thr_ref[...] = _key_to_float(key)


def _indexer(proj, gates, ki, topk, tq, tk):
    s = proj.shape[0]
    n_kt = s // tk
    kernel = functools.partial(_indexer_kernel, tq=tq, tk=tk, topk=topk)
    qi_blk = 6 * 1024 // (IDX_HEADS * IDX_DIM)
    kw_blk = 2048 // LANES
    return pl.pallas_call(
        kernel,
        grid=(s // tq,),
        in_specs=[
            pl.BlockSpec((tq, IDX_HEADS * IDX_DIM), lambda i: (i, qi_blk)),
            pl.BlockSpec((tq, LANES), lambda i: (i, kw_blk)),
            pl.BlockSpec((s, IDX_DIM), lambda i: (0, 0)),
        ],
        out_specs=[
            pl.BlockSpec((1, n_kt, tq, tk), lambda i: (i, 0, 0, 0)),
            pl.BlockSpec((tq, 1), lambda i: (i, 0)),
        ],
        out_shape=[
            jax.ShapeDtypeStruct((s // tq, n_kt, tq, tk), F32),
            jax.ShapeDtypeStruct((s, 1), F32),
        ],
        compiler_params=_cparams(("parallel",)),
        name="dsa_indexer",
    )(proj, gates, ki)


def _dsa_attn_kernel(qi_ref, kj_ref, q_ref, k_ref, v_ref, sc_ref, thr_ref, o_ref,
                     m_ref, l_ref, acc_ref, *, tq, tk):
    p_id = pl.program_id(0)
    i = qi_ref[p_id]
    j = kj_ref[p_id]

    @pl.when(j == 0)
    def _():
        m_ref[...] = jnp.full(m_ref.shape, NEG_BIG, F32)
        l_ref[...] = jnp.zeros(l_ref.shape, F32)
        acc_ref[...] = jnp.zeros(acc_ref.shape, F32)

    sc = jnp.concatenate([sc_ref[a, 0] for a in range(sc_ref.shape[0])], axis=0)
    mask_bias = jnp.where(sc >= thr_ref[...], 0.0, NEG_BIG)
    row = lax.broadcasted_iota(jnp.int32, (tq, tk), 0)
    col = lax.broadcasted_iota(jnp.int32, (tq, tk), 1)
    dist = jnp.abs(row - col + (i * tq - j * tk)).astype(F32)
    scale = B_HEAD_DIM ** -0.5

    for h in range(B_HEADS):
        sl = slice(h * B_HEAD_DIM, (h + 1) * B_HEAD_DIM)
        slope = 2.0 ** (-8.0 * (h + 1) / B_HEADS)
        s = _dot_nt(q_ref[:, sl], k_ref[:, sl]) * scale + (mask_bias - slope * dist)
        m_old = m_ref[h]
        m_new = jnp.maximum(m_old, jnp.max(s, axis=-1, keepdims=True))
        alpha = jnp.exp(m_old - m_new)
        p = jnp.exp(s - m_new)
        l_ref[h] = alpha * l_ref[h] + jnp.sum(p, axis=-1, keepdims=True)
        acc_ref[h] = alpha * acc_ref[h] + jnp.dot(p.astype(BF16), v_ref[:, sl], preferred_element_type=F32)
        m_ref[h] = m_new

    @pl.when(j == i)
    def _():
        for h in range(B_HEADS):
            sl = slice(h * B_HEAD_DIM, (h + 1) * B_HEAD_DIM)
            o_ref[:, sl] = (acc_ref[h] / l_ref[h]).astype(o_ref.dtype)


def _dsa_attention(proj, scores, thr, tq, tk):
    s = proj.shape[0]
    nq = s // tq
    sub = tq // scores.shape[2]
    pairs_i = [i for i in range(nq) for j in range(i + 1)]
    pairs_j = [j for i in range(nq) for j in range(i + 1)]
    qi = jnp.asarray(pairs_i, jnp.int32)
    kj = jnp.asarray(pairs_j, jnp.int32)
    width = B_HEADS * B_HEAD_DIM
    kernel = functools.partial(_dsa_attn_kernel, tq=tq, tk=tk)
    grid_spec = pltpu.PrefetchScalarGridSpec(
        num_scalar_prefetch=2,
        grid=(len(pairs_i),),
        in_specs=[
            pl.BlockSpec((tq, width), lambda p, qi, kj: (qi[p], 3)),
            pl.BlockSpec((tk, width), lambda p, qi, kj: (kj[p], 4)),
            pl.BlockSpec((tk, width), lambda p, qi, kj: (kj[p], 5)),
            pl.BlockSpec((sub, 1, scores.shape[2], tk), lambda p, qi, kj: (qi[p], kj[p], 0, 0)),
            pl.BlockSpec((tq, 1), lambda p, qi, kj: (qi[p], 0)),
        ],
        out_specs=pl.BlockSpec((tq, width), lambda p, qi, kj: (qi[p], 0)),
        scratch_shapes=[
            pltpu.VMEM((B_HEADS, tq, 1), F32),
            pltpu.VMEM((B_HEADS, tq, 1), F32),
            pltpu.VMEM((B_HEADS, tq, B_HEAD_DIM), F32),
        ],
    )
    return pl.pallas_call(
        kernel,
        grid_spec=grid_spec,
        out_shape=jax.ShapeDtypeStruct((s, width), BF16),
        compiler_params=_cparams(("arbitrary",)),
        name="dsa_attn",
    )(qi, kj, proj, proj, proj, scores, thr)


def _merge_router_kernel(ya_ref, yb_ref, ga_ref, gb_ref, x_ref, wa_ref, wb_ref, wo_ref,
                         vec_ref, wr_ref, br_ref, x1_ref, v_ref, ew_ref, ei_ref):
    a = jnp.dot(ya_ref[...], wa_ref[...], preferred_element_type=F32)
    b = jnp.dot(yb_ref[...], wb_ref[...], preferred_element_type=F32)
    merged = jax.nn.sigmoid(ga_ref[...]) * a + jax.nn.sigmoid(gb_ref[...]) * b
    mix = jnp.dot(merged.astype(BF16), wo_ref[...], preferred_element_type=F32)
    g_a, ln_g, ln_b, sc_f, sh_f = (vec_ref[r:r + 1, :] for r in range(5))
    x1 = _layer_norm_rows(DEEPNORM_ALPHA * x_ref[...] + g_a * mix) * ln_g + ln_b
    x1_ref[...] = x1
    v = (_layer_norm_rows(x1) * (1.0 + sc_f) + sh_f).astype(BF16)
    v_ref[...] = v

    logits = jnp.dot(v, wr_ref[...], preferred_element_type=F32) + br_ref[...]
    lane = lax.broadcasted_iota(jnp.int32, logits.shape, 1)
    work = logits
    ew = jnp.zeros(logits.shape, F32)
    ei = jnp.zeros(logits.shape, jnp.int32)
    top = None
    denom = None
    for k in range(TOP_K):
        mx = jnp.max(work, axis=-1, keepdims=True)
        ix = jnp.min(jnp.where(work == mx, lane, LANES), axis=-1, keepdims=True)
        work = jnp.where(lane == ix, -jnp.inf, work)
        if k == 0:
            top = mx
        e = jnp.exp(mx - top)
        denom = e if k == 0 else denom + e
        ew = jnp.where(lane == k, e, ew)
        ei = jnp.where(lane == k, ix, ei)
    ew_ref[...] = ew / denom
    ei_ref[...] = ei


def _merge_router(ya, yb, gates, x, wa, wb, wo, vecs, wr, br, tm):
    s, d = x.shape
    row = lambda i: (i, 0)
    const = lambda i: (0, 0)
    return pl.pallas_call(
        _merge_router_kernel,
        grid=(s // tm,),
        in_specs=[
            pl.BlockSpec((tm, d), row),
            pl.BlockSpec((tm, d), row),
            pl.BlockSpec((tm, d), lambda i: (i, 0)),
            pl.BlockSpec((tm, d), lambda i: (i, 1)),
            pl.BlockSpec((tm, d), row),
            pl.BlockSpec((d, d), const),
            pl.BlockSpec((d, d), const),
            pl.BlockSpec((d, d), const),
            pl.BlockSpec((8, d), const),
            pl.BlockSpec((d, LANES), const),
            pl.BlockSpec((1, LANES), const),
        ],
        out_specs=[
            pl.BlockSpec((tm, d), row),
            pl.BlockSpec((tm, d), row),
            pl.BlockSpec((tm, LANES), row),
            pl.BlockSpec((tm, LANES), row),
        ],
        out_shape=[
            jax.ShapeDtypeStruct((s, d), F32),
            jax.ShapeDtypeStruct((s, d), BF16),
            jax.ShapeDtypeStruct((s, LANES), F32),
            jax.ShapeDtypeStruct((s, LANES), jnp.int32),
        ],
        compiler_params=_cparams(("parallel",)),
        name="merge_router",
    )(ya, yb, gates, gates, x, wa, wb, wo, vecs, wr, br)


def _expert_mlp_kernel(te_ref, tv_ref, x_ref, g_ref, w1_ref, b1_ref, w2_ref, b2_ref, y_ref, *, d_ff):
    t = pl.program_id(0)

    @pl.when(tv_ref[t] > 0)
    def _():
        u = jnp.dot(x_ref[...], w1_ref[0], preferred_element_type=F32) + b1_ref[0]
        x_glu = jnp.minimum(u[:, :d_ff], SWIGLU_LIMIT)
        x_lin = jnp.clip(u[:, d_ff:], -SWIGLU_LIMIT, SWIGLU_LIMIT)
        g = g_ref[...]
        a = x_glu * jax.nn.sigmoid(SWIGLU_ALPHA * x_glu) * (x_lin + 1.0) * g
        y_ref[...] = jnp.dot(a.astype(BF16), w2_ref[0], preferred_element_type=F32) + g * b2_ref[0]

    @pl.when(tv_ref[t] == 0)
    def _():
        y_ref[...] = jnp.zeros(y_ref.shape, F32)


def _expert_mlp(tile_expert, tile_valid, x_sorted, g_sorted, w1, b1, w2, b2, tile):
    n_slot, d = x_sorted.shape
    d_ff = w2.shape[1]
    kernel = functools.partial(_expert_mlp_kernel, d_ff=d_ff)
    grid_spec = pltpu.PrefetchScalarGridSpec(
        num_scalar_prefetch=2,
        grid=(n_slot // tile,),
        in_specs=[
            pl.BlockSpec((tile, d), lambda t, te, tv: (t, 0)),
            pl.BlockSpec((tile, 1), lambda t, te, tv: (t, 0)),
            pl.BlockSpec((1, d, 2 * d_ff), lambda t, te, tv: (te[t], 0, 0)),
            pl.BlockSpec((1, 1, 2 * d_ff), lambda t, te, tv: (te[t], 0, 0)),
            pl.BlockSpec((1, d_ff, d), lambda t, te, tv: (te[t], 0, 0)),
            pl.BlockSpec((1, 1, d), lambda t, te, tv: (te[t], 0, 0)),
        ],
        out_specs=pl.BlockSpec((tile, d), lambda t, te, tv: (t, 0)),
    )
    return pl.pallas_call(
        kernel,
        grid_spec=grid_spec,
        out_shape=jax.ShapeDtypeStruct((n_slot, d), F32),
        compiler_params=_cparams(("arbitrary",)),
        name="expert_mlp",
    )(tile_expert, tile_valid, x_sorted, g_sorted, w1, b1, w2, b2)


def _final_ln_kernel(x1_ref, y_ref, vec_ref, o_ref):
    g_f, ln_g, ln_b = (vec_ref[r:r + 1, :] for r in range(3))
    y = y_ref[0] + y_ref[1] + y_ref[2] + y_ref[3]
    o_ref[...] = _layer_norm_rows(DEEPNORM_ALPHA * x1_ref[...] + g_f * y) * ln_g + ln_b


def _final_ln(x1, y4, vecs, tm):
    s, d = x1.shape
    return pl.pallas_call(
        _final_ln_kernel,
        grid=(s // tm,),
        in_specs=[
            pl.BlockSpec((tm, d), lambda i: (i, 0)),
            pl.BlockSpec((TOP_K, tm, d), lambda i: (0, i, 0)),
            pl.BlockSpec((8, d), lambda i: (0, 0)),
        ],
        out_specs=pl.BlockSpec((tm, d), lambda i: (i, 0)),
        out_shape=jax.ShapeDtypeStruct((s, d), F32),
        compiler_params=_cparams(("parallel",)),
        name="combine_ln",
    )(x1, y4, vecs)


def _routing(eidx, ew, tile):
    s = eidx.shape[0]
    n_assign = s * TOP_K
    n_slot = n_assign + N_EXPERTS * tile
    n_tiles = n_slot // tile
    e_flat = eidx.reshape(-1)
    order = jnp.argsort(e_flat, stable=True).astype(jnp.int32)
    sorted_e = e_flat[order]
    counts = jnp.sum(jax.nn.one_hot(e_flat, N_EXPERTS, dtype=jnp.int32), axis=0)
    padded = ((counts + tile - 1) // tile) * tile
    pad_end = jnp.cumsum(padded)
    pad_start = pad_end - padded
    raw_start = jnp.cumsum(counts) - counts
    rank = jnp.arange(n_assign, dtype=jnp.int32) - raw_start[sorted_e]
    slot_sorted = (pad_start[sorted_e] + rank).astype(jnp.int32)
    tok_of_slot = jnp.zeros((n_slot,), jnp.int32).at[slot_sorted].set(order // TOP_K)
    gate_of_slot = jnp.zeros((n_slot,), F32).at[slot_sorted].set(ew.reshape(-1)[order])
    slot_of_assign = jnp.zeros((n_assign,), jnp.int32).at[order].set(slot_sorted)
    tile_start = jnp.arange(n_tiles, dtype=jnp.int32) * tile
    tile_expert = jnp.minimum(jnp.searchsorted(pad_end, tile_start, side="right"), N_EXPERTS - 1).astype(jnp.int32)
    tile_valid = (tile_start < pad_end[-1]).astype(jnp.int32)
    return tok_of_slot, gate_of_slot, slot_of_assign.reshape(s, TOP_K), tile_expert, tile_valid


def _pad_rows(v, rows=8):
    return jnp.concatenate([v, jnp.zeros((rows - v.shape[0], v.shape[1]), v.dtype)], axis=0)


def kernel(x, c, w_ada, b_ada, w_in, lam_q1, lam_k1, lam_q2, lam_k2, diff_norm_g, w_branch_a, w_branch_b, w_out, ln1_g, ln1_b, w_router, b_router, w_e1, b_e1, w_e2, b_e2, ln2_g, ln2_b):
    bsz, s, d = x.shape
    assert bsz == 1 and DEPTH == 1
    l = 0
    xs = x[0]

    mod = jnp.matmul(jax.nn.silu(c), w_ada[l]) + b_ada[l]
    sh_a, sc_a, g_a, sh_f, sc_f, g_f = jnp.split(mod, 6, axis=-1)

    lam_init = 0.8 - 0.6 * math.exp(-0.3 * l)
    lam = (jnp.exp(jnp.sum(lam_q1[l] * lam_k1[l])) - jnp.exp(jnp.sum(lam_q2[l] * lam_k2[l])) + lam_init)
    lam = lam.reshape(1).astype(F32)
    slopes = 2.0 ** (-8.0 * jnp.arange(1, A_HEADS + 1, dtype=F32) / A_HEADS)

    w = w_in[l]
    n_main = 7 * 1024
    w_main = w[:, :n_main].astype(BF16)
    n_small = IDX_DIM + IDX_HEADS
    w_gate = jnp.concatenate(
        [w[:, n_main + n_small:], w[:, n_main:n_main + n_small], jnp.zeros((d, LANES - n_small), F32)], axis=1
    ).astype(BF16)

    tm = min(512, s)
    proj = _ln_mod_matmul(xs, sc_a, sh_a, w_main, BF16, tm, 1024)
    gates = _ln_mod_matmul(xs, sc_a, sh_a, w_gate, F32, tm, w_gate.shape[1])
    ki = gates[:, 2 * d:2 * d + IDX_DIM].astype(BF16)

    ya = _diff_attention(proj, slopes, lam, diff_norm_g[l].reshape(1, -1), lam_init, min(512, s))

    topk = min(TOPK_MAX, s // 4)
    t_att = min(512, s)
    scores, thr = _indexer(proj, gates, ki, topk, min(128, s), t_att)
    yb = _dsa_attention(proj, scores, thr, t_att, t_att)

    vec5 = _pad_rows(jnp.concatenate([g_a, ln1_g[l][None], ln1_b[l][None], sc_f, sh_f], axis=0))
    wr = jnp.concatenate([w_router[l], jnp.zeros((d, LANES - N_EXPERTS), F32)], axis=1).astype(BF16)
    br = jnp.concatenate([b_router[l], jnp.full((LANES - N_EXPERTS,), NEG_BIG, F32)])[None]
    x1, v, ew, ei = _merge_router(
        ya, yb, gates, xs, w_branch_a[l].astype(BF16), w_branch_b[l].astype(BF16), w_out[l].astype(BF16),
        vec5, wr, br, min(256, s))

    tile = 256
    tok_of_slot, gate_of_slot, slot_of_assign, tile_expert, tile_valid = _routing(ei[:, :TOP_K], ew[:, :TOP_K], tile)
    x_sorted = jnp.take(v, tok_of_slot, axis=0)
    d_ff = w_e2.shape[2]
    w1 = jnp.concatenate([w_e1[l][:, :, 0::2], w_e1[l][:, :, 1::2]], axis=-1).astype(BF16)
    b1 = jnp.concatenate([b_e1[l][:, 0::2], b_e1[l][:, 1::2]], axis=-1)[:, None, :]
    y_sorted = _expert_mlp(tile_expert, tile_valid, x_sorted, gate_of_slot[:, None], w1, b1,
                           w_e2[l].astype(BF16), b_e2[l][:, None, :], tile)
    y4 = jnp.take(y_sorted, slot_of_assign.T, axis=0)

    vec3 = _pad_rows(jnp.concatenate([g_f, ln2_g[l][None], ln2_b[l][None]], axis=0))
    out = _final_ln(x1, y4, vec3, min(512, s))
    return out[None]
```

```python
import functools
import math

import jax
import jax.numpy as jnp
from jax import lax
from jax.experimental import pallas as pl
from jax.experimental.pallas import tpu as pltpu

F32 = jnp.float32
BF16 = jnp.bfloat16

CHUNK = 64
A_HEADS = 8
A_HEAD_DIM = 64
B_HEADS = 8
B_HEAD_DIM = 128
IDX_HEADS = 16
IDX_DIM = 64
TOPK_MAX = 256
N_EXPERTS = 32
TOP_K = 4
SWIGLU_ALPHA = 1.702
SWIGLU_LIMIT = 7.0
LN_EPS = 1e-5
DEPTH = 1
DEEPNORM_ALPHA = (2.0 * DEPTH) ** 0.25

LANES = 128
LOG2E = 1.4426950408889634
NEG_BIG = -1e30
KEY_SIGN = -2147483648
KEY_MANT = 2147483647
KEY_MIN_FINITE = -2139095040
VMEM_LIMIT = 56 * 1024 * 1024


def _cparams(sem):
    return pltpu.CompilerParams(dimension_semantics=sem, vmem_limit_bytes=VMEM_LIMIT)


def _dot_nt(a, b):
    return lax.dot_general(a, b, (((1,), (1,)), ((), ())), preferred_element_type=F32)


def _lane_tile(v, width):
    return jnp.concatenate([v] * (width // LANES), axis=1)


def _layer_norm_rows(x):
    mu = jnp.mean(x, axis=-1, keepdims=True)
    xc = x - mu
    var = jnp.mean(xc * xc, axis=-1, keepdims=True)
    return xc * lax.rsqrt(var + LN_EPS)


def _ln_mod_matmul_kernel(x_ref, sc_ref, sh_ref, w_ref, o_ref, u_ref, *, head_major):
    @pl.when(pl.program_id(1) == 0)
    def _():
        y = _layer_norm_rows(x_ref[...])
        u_ref[...] = (y * (1.0 + sc_ref[...]) + sh_ref[...]).astype(BF16)

    res = jnp.dot(u_ref[...], w_ref[...], preferred_element_type=F32).astype(o_ref.dtype)
    if head_major:
        for g in range(o_ref.shape[0]):
            o_ref[g] = res[:, g * LANES:(g + 1) * LANES]
    else:
        o_ref[...] = res


def _ln_mod_matmul(x, sc, sh, w, out_dtype, tm, tn, head_major):
    s, d = x.shape
    n = w.shape[1]
    if head_major:
        out_spec = pl.BlockSpec((tn // LANES, tm, LANES), lambda i, j: (j, i, 0))
        out_shape = jax.ShapeDtypeStruct((n // LANES, s, LANES), out_dtype)
    else:
        out_spec = pl.BlockSpec((tm, tn), lambda i, j: (i, j))
        out_shape = jax.ShapeDtypeStruct((s, n), out_dtype)
    return pl.pallas_call(
        functools.partial(_ln_mod_matmul_kernel, head_major=head_major),
        grid=(s // tm, n // tn),
        in_specs=[
            pl.BlockSpec((tm, d), lambda i, j: (i, 0)),
            pl.BlockSpec((1, d), lambda i, j: (0, 0)),
            pl.BlockSpec((1, d), lambda i, j: (0, 0)),
            pl.BlockSpec((d, tn), lambda i, j: (0, j)),
        ],
        out_specs=out_spec,
        out_shape=out_shape,
        scratch_shapes=[pltpu.VMEM((tm, d), BF16)],
        compiler_params=_cparams(("parallel", "arbitrary")),
        name="ln_mod_proj",
    )(x, sc, sh, w)


def _diff_attn_kernel(slope_ref, lam_ref, q_ref, k_ref, v_ref, kb_ref, g_ref, o_ref,
                      m_ref, l_ref, acc_ref, *, tq, lam_init):
    h = pl.program_id(0)
    i = pl.program_id(1)
    slope2 = slope_ref[h]
    lam = lam_ref[0]

    qs = q_ref[...]
    lane = lax.broadcasted_iota(jnp.int32, qs.shape, 1)
    zero = jnp.zeros_like(qs)
    q_maps = (jnp.where(lane < A_HEAD_DIM, qs, zero), jnp.where(lane >= A_HEAD_DIM, qs, zero))
    ones3 = jnp.where(lane < 3, 1.0, 0.0).astype(BF16)
    q_aug = tuple(jnp.concatenate([qm, ones3], axis=1) for qm in q_maps)
    k_bias = kb_ref[...]

    row = lax.broadcasted_iota(jnp.int32, (tq, tq), 0)
    col = lax.broadcasted_iota(jnp.int32, (tq, tq), 1)
    bias_diag = slope2 * (row - jnp.abs(row - col)).astype(F32)
    allowed = col < (row // CHUNK + 1) * CHUNK

    m_ref[...] = jnp.full(m_ref.shape, NEG_BIG, F32)
    l_ref[...] = jnp.zeros(l_ref.shape, F32)
    acc_ref[...] = jnp.zeros(acc_ref.shape, F32)

    def step(j, diag):
        start = pl.multiple_of(j * tq, tq)
        k = k_ref[pl.ds(start, tq), :]
        v = v_ref[pl.ds(start, tq), :]
        shift = (-slope2) * ((i - j) * tq).astype(F32)
        if not diag:
            k = jnp.concatenate([k, k_bias], axis=1)
        for c in range(2):
            if diag:
                s = jnp.where(allowed, _dot_nt(q_maps[c], k) + bias_diag, NEG_BIG)
            else:
                s = _dot_nt(q_aug[c], k)
            m_old = m_ref[c]
            m_new = jnp.maximum(m_old, jnp.max(s, axis=-1, keepdims=True) + shift)
            alpha = jnp.exp2(m_old - m_new)
            p = jnp.exp2(s - _lane_tile(m_new - shift, tq))
            l_ref[c] = alpha * l_ref[c] + jnp.sum(p, axis=-1, keepdims=True)
            acc_ref[c] = alpha * acc_ref[c] + jnp.dot(p.astype(BF16), v, preferred_element_type=F32)
            m_ref[c] = m_new

    def body(j, carry):
        step(j, False)
        return carry

    lax.fori_loop(0, i, body, 0)
    step(i, True)

    o = acc_ref[0] / l_ref[0] - lam * (acc_ref[1] / l_ref[1])
    o = o * lax.rsqrt(jnp.mean(o * o, axis=-1, keepdims=True) + LN_EPS)
    o = o * g_ref[...] * (1.0 - lam_init)
    o_ref[...] = o.astype(o_ref.dtype)


def _diff_attention(proj, key_bias, slopes2, lam, norm_g, lam_init, tq):
    s = proj.shape[1]
    hd = 2 * A_HEAD_DIM
    kernel = functools.partial(_diff_attn_kernel, tq=tq, lam_init=lam_init)
    return pl.pallas_call(
        kernel,
        grid=(A_HEADS, s // tq),
        in_specs=[
            pl.BlockSpec(memory_space=pltpu.SMEM),
            pl.BlockSpec(memory_space=pltpu.SMEM),
            pl.BlockSpec((None, tq, hd), lambda h, i: (h, i, 0)),
            pl.BlockSpec((None, s, hd), lambda h, i: (A_HEADS + h, 0, 0)),
            pl.BlockSpec((None, s, hd), lambda h, i: (2 * A_HEADS + h, 0, 0)),
            pl.BlockSpec((None, tq, LANES), lambda h, i: (h, 0, 0)),
            pl.BlockSpec((1, hd), lambda h, i: (0, 0)),
        ],
        out_specs=pl.BlockSpec((tq, hd), lambda h, i: (i, h)),
        out_shape=jax.ShapeDtypeStruct((s, A_HEADS * hd), BF16),
        scratch_shapes=[
            pltpu.VMEM((2, tq, LANES), F32),
            pltpu.VMEM((2, tq, LANES), F32),
            pltpu.VMEM((2, tq, hd), F32),
        ],
        compiler_params=_cparams(("parallel", "arbitrary")),
        name="diff_attn",
    )(slopes2, lam, proj, proj, proj, key_bias, norm_g)


def _key_to_float(key):
    bits = jnp.where(key >= 0, key, key ^ KEY_MANT)
    return lax.bitcast_convert_type(bits, F32)


def _indexer_kernel(qi_ref, kw_ref, ki_ref, s_ref, thr_ref, *, tq, tk, topk):
    i = pl.program_id(0)
    n_all = s_ref.shape[1]
    n_tiles = ((i + 1) * tq + tk - 1) // tk

    w = kw_ref[:, IDX_DIM:IDX_DIM + IDX_HEADS]
    per_group = LANES // IDX_DIM
    q_heads = [qi_ref[h // per_group][:, (h % per_group) * IDX_DIM:(h % per_group + 1) * IDX_DIM]
               for h in range(IDX_HEADS)]
    w_heads = [w[:, h:h + 1] for h in range(IDX_HEADS)]

    qpos = i * tq + lax.broadcasted_iota(jnp.int32, (tq, tk), 0)
    qend = (qpos // CHUNK + 1) * CHUNK
    kcol = lax.broadcasted_iota(jnp.int32, (tq, tk), 1)

    def score_tile(jb, carry):
        start = pl.multiple_of(jb * tk, tk)
        kt = ki_ref[pl.ds(start, tk), :]
        acc = jnp.zeros((tq, tk), F32)
        for h in range(IDX_HEADS):
            acc = acc + w_heads[h] * jnp.maximum(_dot_nt(q_heads[h], kt), 0.0)
        s_ref[0, jb] = acc
        return carry

    lax.fori_loop(0, n_tiles - 1, score_tile, 0)
    last = n_tiles - 1
    score_tile(last, 0)
    s_ref[0, last] = jnp.where(kcol + last * tk < qend, s_ref[0, last], -jnp.inf)

    def fill_tile(jb, carry):
        s_ref[0, jb] = jnp.full((tq, tk), -jnp.inf, F32)
        return carry

    lax.fori_loop(n_tiles, n_all, fill_tile, 0)

    def count_ge(cand):
        cb = jnp.broadcast_to(cand, (tq, LANES))

        def body(jb, acc):
            t = s_ref[0, jb]
            for c in range(tk // LANES):
                acc = acc + jnp.where(t[:, c * LANES:(c + 1) * LANES] >= cb, 1.0, 0.0)
            return acc

        acc = lax.fori_loop(0, n_tiles, body, jnp.zeros((tq, LANES), F32))
        return jnp.sum(acc, axis=-1, keepdims=True)

    def bit_step(b, u):
        trial = u | jnp.left_shift(jnp.int32(1), 31 - b)
        cnt = count_ge(_key_to_float(trial ^ KEY_SIGN))
        return jnp.where(cnt >= float(topk), trial, u)

    u = lax.fori_loop(0, 32, bit_step, jnp.zeros((tq, 1), jnp.int32))
    key = jnp.maximum(u ^ KEY_SIGN, KEY_MIN_FINITE)
    thr_ref[...] = _key_to_float(key)


def _indexer(proj, gates, ki, topk, tq, tk):
    s = proj.shape[1]
    n_kt = s // tk
    kernel = functools.partial(_indexer_kernel, tq=tq, tk=tk, topk=topk)
    qi_groups = IDX_HEADS * IDX_DIM // LANES
    qi_blk = 6
    kw_blk = 2048 // LANES
    return pl.pallas_call(
        kernel,
        grid=(s // tq,),
        in_specs=[
            pl.BlockSpec((qi_groups, tq, LANES), lambda i: (qi_blk, i, 0)),
            pl.BlockSpec((tq, LANES), lambda i: (i, kw_blk)),
            pl.BlockSpec((s, IDX_DIM), lambda i: (0, 0)),
        ],
        out_specs=[
            pl.BlockSpec((1, n_kt, tq, tk), lambda i: (i, 0, 0, 0)),
            pl.BlockSpec((tq, 1), lambda i: (i, 0)),
        ],
        out_shape=[
            jax.ShapeDtypeStruct((s // tq, n_kt, tq, tk), F32),
            jax.ShapeDtypeStruct((s, 1), F32),
        ],
        compiler_params=_cparams(("parallel",)),
        name="dsa_indexer",
    )(proj, gates, ki)


def _dsa_attn_kernel(qi_ref, kj_ref, slope_ref, q_ref, k_ref, v_ref, kb_ref, sc_ref, thr_ref, o_ref,
                     m_ref, l_ref, acc_ref, mb_ref, *, tq, tk):
    p_id = pl.program_id(0)
    i = qi_ref[p_id]
    j = kj_ref[p_id]

    @pl.when(j == 0)
    def _():
        m_ref[...] = jnp.full(m_ref.shape, NEG_BIG, F32)
        l_ref[...] = jnp.zeros(l_ref.shape, F32)
        acc_ref[...] = jnp.zeros(acc_ref.shape, F32)

    sc = jnp.concatenate([sc_ref[a, 0] for a in range(sc_ref.shape[0])], axis=0)
    mb_ref[0] = jnp.where(sc >= thr_ref[...], 0.0, NEG_BIG)
    lane = lax.broadcasted_iota(jnp.int32, (tq, LANES), 1)
    ones3 = jnp.where(lane < 3, 1.0, 0.0).astype(BF16)
    base = ((i - j) * tq).astype(F32)

    def head_step(h, diag):
        slope2 = slope_ref[h]
        q_aug = jnp.concatenate([q_ref[h], ones3], axis=1)
        k_aug = jnp.concatenate([k_ref[h], kb_ref[h]], axis=1)
        s = _dot_nt(q_aug, k_aug) + mb_ref[0]
        if diag:
            s = s - slope2 * mb_ref[1]
        shift = (-slope2) * base
        m_old = m_ref[h]
        m_new = jnp.maximum(m_old, jnp.max(s, axis=-1, keepdims=True) + shift)
        alpha = jnp.exp2(m_old - m_new)
        p = jnp.exp2(s - _lane_tile(m_new - shift, tk))
        l_ref[h] = alpha * l_ref[h] + jnp.sum(p, axis=-1, keepdims=True)
        acc_ref[h] = alpha * acc_ref[h] + jnp.dot(p.astype(BF16), v_ref[h], preferred_element_type=F32)
        m_ref[h] = m_new

    @pl.when(j < i)
    def _():
        def body(h, carry):
            head_step(h, False)
            return carry

        lax.fori_loop(0, B_HEADS, body, 0)

    @pl.when(j == i)
    def _():
        row = lax.broadcasted_iota(jnp.int32, (tq, tk), 0)
        col = lax.broadcasted_iota(jnp.int32, (tq, tk), 1)
        mb_ref[1] = 2.0 * jnp.maximum(col - row, 0).astype(F32)

        def body(h, carry):
            head_step(h, True)
            return carry

        lax.fori_loop(0, B_HEADS, body, 0)
        for h in range(B_HEADS):
            o_ref[h] = (acc_ref[h] / l_ref[h]).astype(o_ref.dtype)


def _dsa_attention(proj, key_bias, slopes2, scores, thr, tq, tk):
    s = proj.shape[1]
    nq = s // tq
    sub = tq // scores.shape[2]
    pairs_i = [i for i in range(nq) for j in range(i + 1)]
    pairs_j = [j for i in range(nq) for j in range(i + 1)]
    qi = jnp.asarray(pairs_i, jnp.int32)
    kj = jnp.asarray(pairs_j, jnp.int32)
    hd = B_HEAD_DIM
    kernel = functools.partial(_dsa_attn_kernel, tq=tq, tk=tk)
    grid_spec = pltpu.PrefetchScalarGridSpec(
        num_scalar_prefetch=3,
        grid=(len(pairs_i),),
        in_specs=[
            pl.BlockSpec((B_HEADS, tq, hd), lambda p, qi, kj, sl: (3, qi[p], 0)),
            pl.BlockSpec((B_HEADS, tk, hd), lambda p, qi, kj, sl: (4, kj[p], 0)),
            pl.BlockSpec((B_HEADS, tk, hd), lambda p, qi, kj, sl: (5, kj[p], 0)),
            pl.BlockSpec((B_HEADS, tk, LANES), lambda p, qi, kj, sl: (0, 0, 0)),
            pl.BlockSpec((sub, 1, scores.shape[2], tk), lambda p, qi, kj, sl: (qi[p], kj[p], 0, 0)),
            pl.BlockSpec((tq, 1), lambda p, qi, kj, sl: (qi[p], 0)),
        ],
        out_specs=pl.BlockSpec((B_HEADS, tq, hd), lambda p, qi, kj, sl: (0, qi[p], 0)),
        scratch_shapes=[
            pltpu.VMEM((B_HEADS, tq, LANES), F32),
            pltpu.VMEM((B_HEADS, tq, LANES), F32),
            pltpu.VMEM((B_HEADS, tq, hd), F32),
            pltpu.VMEM((2, tq, tk), F32),
        ],
    )
    return pl.pallas_call(
        kernel,
        grid_spec=grid_spec,
        out_shape=jax.ShapeDtypeStruct((B_HEADS, s, hd), BF16),
        compiler_params=_cparams(("arbitrary",)),
        name="dsa_attn",
    )(qi, kj, slopes2, proj, proj, proj, key_bias, scores, thr)


def _merge_router_kernel(ya_ref, yb_ref, ga_ref, gb_ref, x_ref, wa_ref, wb_ref, wo_ref,
                         vec_ref, wr_ref, br_ref, x1_ref, v_ref, ew_ref, ei_ref):
    a = jnp.dot(ya_ref[...], wa_ref[...], preferred_element_type=F32)
    yb = jnp.concatenate([yb_ref[h] for h in range(yb_ref.shape[0])], axis=1)
    b = jnp.dot(yb, wb_ref[...], preferred_element_type=F32)
    merged = jax.nn.sigmoid(ga_ref[...]) * a + jax.nn.sigmoid(gb_ref[...]) * b
    mix = jnp.dot(merged.astype(BF16), wo_ref[...], preferred_element_type=F32)
    g_a, ln_g, ln_b, sc_f, sh_f = (vec_ref[r:r + 1, :] for r in range(5))
    x1 = _layer_norm_rows(DEEPNORM_ALPHA * x_ref[...] + g_a * mix) * ln_g + ln_b
    x1_ref[...] = x1
    v = (_layer_norm_rows(x1) * (1.0 + sc_f) + sh_f).astype(BF16)
    v_ref[...] = v

    logits = jnp.dot(v, wr_ref[...], preferred_element_type=F32) + br_ref[...]
    lane = lax.broadcasted_iota(jnp.int32, logits.shape, 1)
    work = logits
    ew = jnp.zeros(logits.shape, F32)
    ei = jnp.zeros(logits.shape, jnp.int32)
    top = None
    denom = None
    for k in range(TOP_K):
        mx = jnp.max(work, axis=-1, keepdims=True)
        ix = jnp.min(jnp.where(work == mx, lane, LANES), axis=-1, keepdims=True)
        work = jnp.where(lane == ix, -jnp.inf, work)
        if k == 0:
            top = mx
        e = jnp.exp(mx - top)
        denom = e if k == 0 else denom + e
        ew = jnp.where(lane == k, e, ew)
        ei = jnp.where(lane == k, ix, ei)
    ew_ref[...] = ew / denom
    ei_ref[...] = ei


def _merge_router(ya, yb, gates, x, wa, wb, wo, vecs, wr, br, tm):
    s, d = x.shape
    row = lambda i: (i, 0)
    const = lambda i: (0, 0)
    return pl.pallas_call(
        _merge_router_kernel,
        grid=(s // tm,),
        in_specs=[
            pl.BlockSpec((tm, d), row),
            pl.BlockSpec((B_HEADS, tm, B_HEAD_DIM), lambda i: (0, i, 0)),
            pl.BlockSpec((tm, d), lambda i: (i, 0)),
            pl.BlockSpec((tm, d), lambda i: (i, 1)),
            pl.BlockSpec((tm, d), row),
            pl.BlockSpec((d, d), const),
            pl.BlockSpec((d, d), const),
            pl.BlockSpec((d, d), const),
            pl.BlockSpec((8, d), const),
            pl.BlockSpec((d, LANES), const),
            pl.BlockSpec((1, LANES), const),
        ],
        out_specs=[
            pl.BlockSpec((tm, d), row),
            pl.BlockSpec((tm, d), row),
            pl.BlockSpec((tm, LANES), row),
            pl.BlockSpec((tm, LANES), row),
        ],
        out_shape=[
            jax.ShapeDtypeStruct((s, d), F32),
            jax.ShapeDtypeStruct((s, d), BF16),
            jax.ShapeDtypeStruct((s, LANES), F32),
            jax.ShapeDtypeStruct((s, LANES), jnp.int32),
        ],
        compiler_params=_cparams(("parallel",)),
        name="merge_router",
    )(ya, yb, gates, gates, x, wa, wb, wo, vecs, wr, br)


def _split_w1_kernel(w_ref, glu_ref, lin_ref):
    w = w_ref[0].astype(BF16)
    n_in, n_out = w.shape[1], glu_ref.shape[2]
    src = lax.broadcasted_iota(jnp.int32, (n_in, n_out), 0)
    dst = lax.broadcasted_iota(jnp.int32, (n_in, n_out), 1)
    pick_even = jnp.where(src == 2 * dst, 1.0, 0.0).astype(BF16)
    pick_odd = jnp.where(src == 2 * dst + 1, 1.0, 0.0).astype(BF16)
    glu_ref[0] = jnp.dot(w, pick_even, preferred_element_type=F32).astype(BF16)
    lin_ref[0] = jnp.dot(w, pick_odd, preferred_element_type=F32).astype(BF16)


def _split_w1(w1, tn=512):
    e, d, n = w1.shape
    out = jax.ShapeDtypeStruct((e, d, n // 2), BF16)
    return pl.pallas_call(
        _split_w1_kernel,
        grid=(e, n // tn),
        in_specs=[pl.BlockSpec((1, d, tn), lambda a, b: (a, 0, b))],
        out_specs=[pl.BlockSpec((1, d, tn // 2), lambda a, b: (a, 0, b))] * 2,
        out_shape=[out, out],
        compiler_params=_cparams(("parallel", "parallel")),
        name="split_w1",
    )(w1)


def _expert_mlp_kernel(te_ref, tv_ref, x_ref, g_ref, w1g_ref, w1l_ref, b1_ref, w2_ref, b2_ref, y_ref, *, d_ff):
    t = pl.program_id(0)

    @pl.when(tv_ref[t] > 0)
    def _():
        x = x_ref[...]
        b1 = b1_ref[0]
        x_glu = jnp.minimum(jnp.dot(x, w1g_ref[0], preferred_element_type=F32) + b1[:, :d_ff], SWIGLU_LIMIT)
        x_lin = jnp.clip(jnp.dot(x, w1l_ref[0], preferred_element_type=F32) + b1[:, d_ff:],
                         -SWIGLU_LIMIT, SWIGLU_LIMIT)
        g = g_ref[...]
        a = x_glu * jax.nn.sigmoid(SWIGLU_ALPHA * x_glu) * (x_lin + 1.0) * g
        y_ref[...] = jnp.dot(a.astype(BF16), w2_ref[0], preferred_element_type=F32) + g * b2_ref[0]

    @pl.when(tv_ref[t] == 0)
    def _():
        y_ref[...] = jnp.zeros(y_ref.shape, F32)


def _expert_mlp(tile_expert, tile_valid, x_sorted, g_sorted, w1g, w1l, b1, w2, b2, tile):
    n_slot, d = x_sorted.shape
    d_ff = w2.shape[1]
    kernel = functools.partial(_expert_mlp_kernel, d_ff=d_ff)
    grid_spec = pltpu.PrefetchScalarGridSpec(
        num_scalar_prefetch=2,
        grid=(n_slot // tile,),
        in_specs=[
            pl.BlockSpec((tile, d), lambda t, te, tv: (t, 0)),
            pl.BlockSpec((tile, 1), lambda t, te, tv: (t, 0)),
            pl.BlockSpec((1, d, d_ff), lambda t, te, tv: (te[t], 0, 0)),
            pl.BlockSpec((1, d, d_ff), lambda t, te, tv: (te[t], 0, 0)),
            pl.BlockSpec((1, 1, 2 * d_ff), lambda t, te, tv: (te[t], 0, 0)),
            pl.BlockSpec((1, d_ff, d), lambda t, te, tv: (te[t], 0, 0)),
            pl.BlockSpec((1, 1, d), lambda t, te, tv: (te[t], 0, 0)),
        ],
        out_specs=pl.BlockSpec((tile, d), lambda t, te, tv: (t, 0)),
    )
    return pl.pallas_call(
        kernel,
        grid_spec=grid_spec,
        out_shape=jax.ShapeDtypeStruct((n_slot, d), F32),
        compiler_params=_cparams(("arbitrary",)),
        name="expert_mlp",
    )(tile_expert, tile_valid, x_sorted, g_sorted, w1g, w1l, b1, w2, b2)


def _final_ln_kernel(x1_ref, y_ref, vec_ref, o_ref):
    g_f, ln_g, ln_b = (vec_ref[r:r + 1, :] for r in range(3))
    y = y_ref[0] + y_ref[1] + y_ref[2] + y_ref[3]
    o_ref[...] = _layer_norm_rows(DEEPNORM_ALPHA * x1_ref[...] + g_f * y) * ln_g + ln_b


def _final_ln(x1, y4, vecs, tm):
    s, d = x1.shape
    return pl.pallas_call(
        _final_ln_kernel,
        grid=(s // tm,),
        in_specs=[
            pl.BlockSpec((tm, d), lambda i: (i, 0)),
            pl.BlockSpec((TOP_K, tm, d), lambda i: (0, i, 0)),
            pl.BlockSpec((8, d), lambda i: (0, 0)),
        ],
        out_specs=pl.BlockSpec((tm, d), lambda i: (i, 0)),
        out_shape=jax.ShapeDtypeStruct((s, d), F32),
        compiler_params=_cparams(("parallel",)),
        name="combine_ln",
    )(x1, y4, vecs)


def _routing(eidx, ew, tile):
    s = eidx.shape[0]
    n_assign = s * TOP_K
    n_slot = n_assign + N_EXPERTS * tile
    n_tiles = n_slot // tile
    e_flat = eidx.reshape(-1)
    order = jnp.argsort(e_flat, stable=True).astype(jnp.int32)
    sorted_e = e_flat[order]
    counts = jnp.sum(jax.nn.one_hot(e_flat, N_EXPERTS, dtype=jnp.int32), axis=0)
    padded = ((counts + tile - 1) // tile) * tile
    pad_end = jnp.cumsum(padded)
    pad_start = pad_end - padded
    raw_start = jnp.cumsum(counts) - counts
    rank = jnp.arange(n_assign, dtype=jnp.int32) - raw_start[sorted_e]
    slot_sorted = (pad_start[sorted_e] + rank).astype(jnp.int32)
    tok_of_slot = jnp.zeros((n_slot,), jnp.int32).at[slot_sorted].set(order // TOP_K)
    gate_of_slot = jnp.zeros((n_slot,), F32).at[slot_sorted].set(ew.reshape(-1)[order])
    slot_of_assign = jnp.zeros((n_assign,), jnp.int32).at[order].set(slot_sorted)
    tile_start = jnp.arange(n_tiles, dtype=jnp.int32) * tile
    tile_expert = jnp.minimum(jnp.searchsorted(pad_end, tile_start, side="right"), N_EXPERTS - 1).astype(jnp.int32)
    tile_valid = (tile_start < pad_end[-1]).astype(jnp.int32)
    return tok_of_slot, gate_of_slot, slot_of_assign.reshape(s, TOP_K), tile_expert, tile_valid


def _pad_rows(v, rows=8):
    return jnp.concatenate([v, jnp.zeros((rows - v.shape[0], v.shape[1]), v.dtype)], axis=0)


def kernel(x, c, w_ada, b_ada, w_in, lam_q1, lam_k1, lam_q2, lam_k2, diff_norm_g, w_branch_a, w_branch_b, w_out, ln1_g, ln1_b, w_router, b_router, w_e1, b_e1, w_e2, b_e2, ln2_g, ln2_b):
    bsz, s, d = x.shape
    assert bsz == 1 and DEPTH == 1
    l = 0
    xs = x[0]

    mod = jnp.matmul(jax.nn.silu(c), w_ada[l]) + b_ada[l]
    sh_a, sc_a, g_a, sh_f, sc_f, g_f = jnp.split(mod, 6, axis=-1)

    lam_init = 0.8 - 0.6 * math.exp(-0.3 * l)
    lam = (jnp.exp(jnp.sum(lam_q1[l] * lam_k1[l])) - jnp.exp(jnp.sum(lam_q2[l] * lam_k2[l])) + lam_init)
    lam = lam.reshape(1).astype(F32)
    t_att = min(512, s)
    slopes2 = 2.0 ** (-8.0 * jnp.arange(1, A_HEADS + 1, dtype=F32) / A_HEADS) * LOG2E
    kb = slopes2[:, None] * jnp.arange(t_att, dtype=F32)[None, :]
    kb_hi = kb.astype(BF16)
    kb_mid = (kb - kb_hi.astype(F32)).astype(BF16)
    kb_lo = (kb - kb_hi.astype(F32) - kb_mid.astype(F32)).astype(BF16)
    key_bias = jnp.concatenate(
        [jnp.stack([kb_hi, kb_mid, kb_lo], axis=-1), jnp.zeros((A_HEADS, t_att, LANES - 3), BF16)], axis=-1)

    w = w_in[l]
    n_main = 7 * 1024
    col_scale = jnp.ones((n_main,), F32)
    col_scale = col_scale.at[0:1024].set(A_HEAD_DIM ** -0.5 * LOG2E)
    col_scale = col_scale.at[3 * 1024:4 * 1024].set(B_HEAD_DIM ** -0.5 * LOG2E)
    w_main = (w[:, :n_main] * col_scale).astype(BF16)
    n_small = IDX_DIM + IDX_HEADS
    w_gate = jnp.concatenate(
        [w[:, n_main + n_small:], w[:, n_main:n_main + n_small], jnp.zeros((d, LANES - n_small), F32)], axis=1
    ).astype(BF16)

    tm = min(512, s)
    proj = _ln_mod_matmul(xs, sc_a, sh_a, w_main, BF16, tm, 1024, True)
    gates = _ln_mod_matmul(xs, sc_a, sh_a, w_gate, F32, tm, w_gate.shape[1], False)
    ki = gates[:, 2 * d:2 * d + IDX_DIM].astype(BF16)

    ya = _diff_attention(proj, key_bias, slopes2, lam, diff_norm_g[l].reshape(1, -1), lam_init, t_att)

    topk = min(TOPK_MAX, s // 4)
    scores, thr = _indexer(proj, gates, ki, topk, min(128, s), t_att)
    yb = _dsa_attention(proj, key_bias, slopes2, scores, thr, t_att, t_att)

    vec5 = _pad_rows(jnp.concatenate([g_a, ln1_g[l][None], ln1_b[l][None], sc_f, sh_f], axis=0))
    wr = jnp.concatenate([w_router[l], jnp.zeros((d, LANES - N_EXPERTS), F32)], axis=1).astype(BF16)
    br = jnp.concatenate([b_router[l], jnp.full((LANES - N_EXPERTS,), NEG_BIG, F32)])[None]
    x1, v, ew, ei = _merge_router(
        ya, yb, gates, xs, w_branch_a[l].astype(BF16), w_branch_b[l].astype(BF16), w_out[l].astype(BF16),
        vec5, wr, br, min(256, s))

    tile = 256
    tok_of_slot, gate_of_slot, slot_of_assign, tile_expert, tile_valid = _routing(ei[:, :TOP_K], ew[:, :TOP_K], tile)
    x_sorted = jnp.take(v, tok_of_slot, axis=0)
    w1g, w1l = _split_w1(w_e1[l])
    b1 = jnp.concatenate([b_e1[l][:, 0::2], b_e1[l][:, 1::2]], axis=-1)[:, None, :]
    y_sorted = _expert_mlp(tile_expert, tile_valid, x_sorted, gate_of_slot[:, None], w1g, w1l, b1,
                           w_e2[l].astype(BF16), b_e2[l][:, None, :], tile)
    y4 = jnp.take(y_sorted, slot_of_assign.T, axis=0)

    vec3 = _pad_rows(jnp.concatenate([g_f, ln2_g[l][None], ln2_b[l][None]], axis=0))
    out = _final_ln(x1, y4, vec3, min(512, s))
    return out[None]
```
